```python
import jax, jax.numpy as jnp
from jax import lax
import numpy as np

D_MODEL = 2048
BATCH = 2
SEQ = 4096
DEPTH = 1
DEC_BATCH = 128
DEC_SEQ = 8
PAST_LEN = 2048
PAGE_SIZE = 128

SB_HEADS = 8
HEAD_DIM = 128
SB_WIDTH = SB_HEADS * HEAD_DIM
Q_BLOCK = 128
SB_BIAS_INIT = -5.0
GM_GROUPS = 8
GM_CH = 128
GM_WIDTH = GM_GROUPS * GM_CH
CHUNK = 128
MIX_WIDTH = SB_WIDTH + GM_WIDTH
IN_WIDTH = 3 * SB_WIDTH + 2 * GM_WIDTH
N_EXPERTS = 32
TOP_K = 4
D_EXPERT = D_MODEL
SWIGLU_ALPHA = 1.702
SWIGLU_LIMIT = 7.0
MOE_BLOCK = 128
PLE_DIM = 256
EPS = 1e-6

kernel_name = "hymba_stickbreak_gmlp_moe_step"


def rms_norm(x, g):
    xf = x.astype(jnp.float32)
    y = xf * lax.rsqrt(jnp.mean(xf * xf, axis=-1, keepdims=True) + EPS)
    return (y * g.astype(jnp.float32)).astype(x.dtype)


def stick_breaking(q, k, v, bias, q_pos, k_pos):
    z = (jnp.einsum('bqhd,bkhd->bhqk', q, k).astype(jnp.float32) * (HEAD_DIM ** -0.5)
         + bias.astype(jnp.float32)[None, :, None, None])
    mask = k_pos[None, :] < q_pos[:, None]
    log_beta = jnp.where(mask, jax.nn.log_sigmoid(z), -jnp.inf)
    log_1mb = jnp.where(mask, jax.nn.log_sigmoid(-z), 0.0)
    rev = lax.cumsum(log_1mb, axis=3, reverse=True)
    excl = jnp.concatenate([rev[..., 1:], jnp.zeros_like(rev[..., :1])], axis=-1)
    a = jnp.exp(log_beta + excl)
    o = jnp.einsum('bhqk,bkhd->bqhd', a, v.astype(jnp.float32))
    return o.astype(v.dtype)


def sb_prompt(q, k, v, bias):
    s = q.shape[1]
    outs = []
    for i in range(s // Q_BLOCK):
        lo, hi = i * Q_BLOCK, (i + 1) * Q_BLOCK
        outs.append(stick_breaking(q[:, lo:hi], k[:, :hi], v[:, :hi], bias,
                                   jnp.arange(lo, hi), jnp.arange(hi)))
    return jnp.concatenate(outs, axis=1)


def chunk_gmlp(u, gv, w_s, b_s, ln_g, ln_b):
    b, l = u.shape[0], u.shape[1]
    vf = gv.astype(jnp.float32)
    mu = jnp.mean(vf, axis=-1, keepdims=True)
    var = jnp.mean(jnp.square(vf - mu), axis=-1, keepdims=True)
    vn = ((vf - mu) * lax.rsqrt(var + EPS) * ln_g.astype(jnp.float32)
          + ln_b.astype(jnp.float32)).astype(gv.dtype)
    n_chunks = -(-l // CHUNK)
    pad = n_chunks * CHUNK - l
    vp = jnp.pad(vn, ((0, 0), (0, pad), (0, 0), (0, 0))).reshape(b, n_chunks, CHUNK, GM_GROUPS, GM_CH)
    causal = jnp.tril(jnp.ones((CHUNK, CHUNK), dtype=bool))
    w = jnp.where(causal[None], w_s, jnp.zeros_like(w_s))
    mix = jnp.einsum('gts,bnsgc->bntgc', w, vp) + b_s.T[:, :, None]
    mix = mix.reshape(b, n_chunks * CHUNK, GM_GROUPS, GM_CH)[:, :l]
    return u * mix, vn


def moe(x2, w_router, b_router, w1, b1, w2, b2):
    t = x2.shape[0]
    logits = (x2 @ w_router).astype(jnp.float32) + b_router.astype(jnp.float32)
    top_vals, top_idx = lax.top_k(logits, TOP_K)
    gates = jax.nn.softmax(top_vals, axis=-1)
    e_flat = top_idx.reshape(-1).astype(jnp.int32)
    tok_flat = jnp.repeat(jnp.arange(t, dtype=jnp.int32), TOP_K)
    g_flat = gates.reshape(-1)
    order = jnp.argsort(e_flat)
    e_sorted = e_flat[order]
    counts = jnp.zeros((N_EXPERTS,), jnp.int32).at[e_flat].add(1)
    padded = (counts + MOE_BLOCK - 1) // MOE_BLOCK * MOE_BLOCK
    start = jnp.cumsum(counts) - counts
    pend = jnp.cumsum(padded)
    pstart = pend - padded
    rank = jnp.arange(t * TOP_K, dtype=jnp.int32) - start[e_sorted]
    dest = pstart[e_sorted] + rank
    n_rows = (-(-(t * TOP_K) // MOE_BLOCK) + N_EXPERTS) * MOE_BLOCK
    row_tok = jnp.full((n_rows,), t, jnp.int32).at[dest].set(tok_flat[order])
    row_gate = jnp.zeros((n_rows,), jnp.float32).at[dest].set(g_flat[order])
    n_blocks = n_rows // MOE_BLOCK
    block_start = jnp.arange(n_blocks, dtype=jnp.int32) * MOE_BLOCK
    block_exp = jnp.clip(jnp.searchsorted(pend, block_start, side='right'), 0, N_EXPERTS - 1)
    x_pad = jnp.concatenate([x2, jnp.zeros((1, x2.shape[1]), x2.dtype)], axis=0)
    xb = x_pad[row_tok].reshape(n_blocks, MOE_BLOCK, x2.shape[1])

    def expert_block(args):
        xblk, e = args
        h = xblk @ w1[e] + b1[e]
        glu = jnp.minimum(h[:, :D_EXPERT], SWIGLU_LIMIT)
        lin = jnp.clip(h[:, D_EXPERT:], -SWIGLU_LIMIT, SWIGLU_LIMIT)
        act = glu * jax.nn.sigmoid(SWIGLU_ALPHA * glu) * (lin + 1)
        return act @ w2[e] + b2[e]

    yb = lax.map(expert_block, (xb, block_exp)).reshape(n_rows, x2.shape[1])
    y = jax.ops.segment_sum(yb.astype(jnp.float32) * row_gate[:, None], row_tok, num_segments=t + 1)[:t]
    return y.astype(x2.dtype)


def layer(h, p, attn_fn, g_mix, w_in, sb_bias, gm_ln_g, gm_ln_b, gm_w_s, gm_b_s, g_sb_out, g_gm_out,
          w_out, g_ffn, w_router, b_router, w_exp1, b_exp1, w_exp2, b_exp2, g_ple, w_ple_gate, w_ple_proj):
    b, l, d = h.shape
    a = rms_norm(h, g_mix)
    z = a @ w_in
    q = z[..., :SB_WIDTH].reshape(b, l, SB_HEADS, HEAD_DIM)
    k = z[..., SB_WIDTH:2 * SB_WIDTH].reshape(b, l, SB_HEADS, HEAD_DIM)
    v = z[..., 2 * SB_WIDTH:3 * SB_WIDTH].reshape(b, l, SB_HEADS, HEAD_DIM)
    gz = jax.nn.gelu(z[..., 3 * SB_WIDTH:])
    u = gz[..., :GM_WIDTH].reshape(b, l, GM_GROUPS, GM_CH)
    gv = gz[..., GM_WIDTH:].reshape(b, l, GM_GROUPS, GM_CH)
    sb_o = attn_fn(q, k, v, sb_bias).reshape(b, l, SB_WIDTH)
    gm_o, gm_vn = chunk_gmlp(u, gv, gm_w_s, gm_b_s, gm_ln_g, gm_ln_b)
    mixed = jnp.concatenate([rms_norm(sb_o, g_sb_out),
                             rms_norm(gm_o.reshape(b, l, GM_WIDTH), g_gm_out)], axis=-1)
    h = h + mixed @ w_out
    f = rms_norm(h, g_ffn)
    h = h + moe(f.reshape(b * l, d), w_router, b_router, w_exp1, b_exp1, w_exp2, b_exp2).reshape(b, l, d)
    gate = jax.nn.sigmoid(rms_norm(h, g_ple) @ w_ple_gate)
    h = h + (p @ w_ple_proj) * gate
    return h, k, v, gm_vn


def setup_inputs(seed: int = 0) -> dict:
    key = jax.random.key(seed)
    ks = jax.random.split(key, 32)
    f32 = jnp.float32

    def nrm(k, shape, scale):
        return jax.random.normal(k, shape, f32) * scale

    n_pages = PAST_LEN // PAGE_SIZE
    n_phys = (DEC_BATCH * n_pages * 5) // 4
    page_table = jax.random.permutation(ks[5], n_phys)[:DEC_BATCH * n_pages].reshape(DEC_BATCH, n_pages).astype(jnp.int32)
    return {
        "x_prompt": nrm(ks[0], (BATCH, SEQ, D_MODEL), 1.0),
        "x_sample": nrm(ks[1], (DEC_BATCH, DEC_SEQ, D_MODEL), 1.0),
        "cache_k": nrm(ks[2], (DEPTH, n_phys, PAGE_SIZE, SB_HEADS, HEAD_DIM), 1.0),
        "cache_v": nrm(ks[3], (DEPTH, n_phys, PAGE_SIZE, SB_HEADS, HEAD_DIM), 1.0),
        "page_table": page_table,
        "p_prompt": nrm(ks[4], (DEPTH, BATCH, SEQ, PLE_DIM), 1.0),
        "p_sample": nrm(ks[6], (DEPTH, DEC_BATCH, DEC_SEQ, PLE_DIM), 1.0),
        "g_mix": 1.0 + nrm(ks[7], (DEPTH, D_MODEL), 0.02),
        "w_in": nrm(ks[8], (DEPTH, D_MODEL, IN_WIDTH), D_MODEL ** -0.5),
        "sb_bias": SB_BIAS_INIT + nrm(ks[27], (DEPTH, SB_HEADS), 0.1),
        "gm_ln_g": 1.0 + nrm(ks[9], (DEPTH, GM_GROUPS, GM_CH), 0.02),
        "gm_ln_b": nrm(ks[10], (DEPTH, GM_GROUPS, GM_CH), 0.02),
        "gm_w_s": nrm(ks[11], (DEPTH, GM_GROUPS, CHUNK, CHUNK), CHUNK ** -0.5),
        "gm_b_s": 1.0 + nrm(ks[12], (DEPTH, GM_GROUPS, CHUNK), 0.1),
        "g_sb_out": 1.0 + nrm(ks[13], (DEPTH, SB_WIDTH), 0.02),
        "g_gm_out": 1.0 + nrm(ks[14], (DEPTH, GM_WIDTH), 0.02),
        "w_out": nrm(ks[15], (DEPTH, MIX_WIDTH, D_MODEL), MIX_WIDTH ** -0.5),
        "g_ffn": 1.0 + nrm(ks[16], (DEPTH, D_MODEL), 0.02),
        "w_router": nrm(ks[17], (DEPTH, D_MODEL, N_EXPERTS), D_MODEL ** -0.5),
        "b_router": nrm(ks[18], (DEPTH, N_EXPERTS), 0.01),
        "w_exp1": nrm(ks[19], (DEPTH, N_EXPERTS, D_MODEL, 2 * D_EXPERT), D_MODEL ** -0.5),
        "b_exp1": nrm(ks[20], (DEPTH, N_EXPERTS, 2 * D_EXPERT), 0.02),
        "w_exp2": nrm(ks[21], (DEPTH, N_EXPERTS, D_EXPERT, D_MODEL), D_EXPERT ** -0.5),
        "b_exp2": nrm(ks[22], (DEPTH, N_EXPERTS, D_MODEL), 0.02),
        "g_ple": 1.0 + nrm(ks[23], (DEPTH, D_MODEL), 0.02),
        "w_ple_gate": nrm(ks[24], (DEPTH, D_MODEL, D_MODEL), D_MODEL ** -0.5),
        "w_ple_proj": nrm(ks[25], (DEPTH, PLE_DIM, D_MODEL), PLE_DIM ** -0.5),
        "g_final": 1.0 + nrm(ks[26], (D_MODEL,), 0.02),
    }


def reference(x_prompt, x_sample, cache_k, cache_v, page_table, p_prompt, p_sample,
              g_mix, w_in, sb_bias, gm_ln_g, gm_ln_b, gm_w_s, gm_b_s, g_sb_out, g_gm_out, w_out,
              g_ffn, w_router, b_router, w_exp1, b_exp1, w_exp2, b_exp2,
              g_ple, w_ple_gate, w_ple_proj, g_final):
    hp, hs = x_prompt, x_sample
    kp_l, vp_l, ks_l, vs_l, gm_l = [], [], [], [], []
    dec_b, n_pages = page_table.shape
    past_len = n_pages * PAGE_SIZE
    for i in range(DEPTH):
        w = (g_mix[i], w_in[i], sb_bias[i], gm_ln_g[i], gm_ln_b[i], gm_w_s[i], gm_b_s[i], g_sb_out[i], g_gm_out[i],
             w_out[i], g_ffn[i], w_router[i], b_router[i], w_exp1[i], b_exp1[i], w_exp2[i], b_exp2[i],
             g_ple[i], w_ple_gate[i], w_ple_proj[i])
        hp, kp, vp, _ = layer(hp, p_prompt[i], sb_prompt, *w)
        past_k = cache_k[i][page_table].reshape(dec_b, past_len, SB_HEADS, HEAD_DIM)
        past_v = cache_v[i][page_table].reshape(dec_b, past_len, SB_HEADS, HEAD_DIM)

        def sb_sample(q, k, v, bias, past_k=past_k, past_v=past_v):
            n = q.shape[1]
            k_all = jnp.concatenate([past_k.astype(k.dtype), k], axis=1)
            v_all = jnp.concatenate([past_v.astype(v.dtype), v], axis=1)
            return stick_breaking(q, k_all, v_all, bias, past_len + jnp.arange(n), jnp.arange(past_len + n))

        hs, ksn, vsn, gmv = layer(hs, p_sample[i], sb_sample, *w)
        kp_l.append(kp); vp_l.append(vp); ks_l.append(ksn); vs_l.append(vsn); gm_l.append(gmv)
    y_prompt = rms_norm(hp, g_final)
    y_sample = rms_norm(hs, g_final)
    k_prompt = jnp.stack(kp_l, axis=0)
    v_prompt = jnp.stack(vp_l, axis=0)
    k_sample = jnp.stack(ks_l, axis=0)
    v_sample = jnp.stack(vs_l, axis=0)
    gm_v_sample = jnp.stack(gm_l, axis=0)
    return (y_prompt, y_sample, k_prompt, v_prompt, k_sample, v_sample, gm_v_sample)
```

```python
import functools

import jax
import jax.numpy as jnp
from jax import lax
from jax.experimental import pallas as pl
from jax.experimental.pallas import tpu as pltpu

D_MODEL = 2048
SB_HEADS = 8
HEAD_DIM = 128
SB_WIDTH = SB_HEADS * HEAD_DIM
GM_GROUPS = 8
GM_CH = 128
GM_WIDTH = GM_GROUPS * GM_CH
CHUNK = 128
PAGE_SIZE = 128
N_SECTIONS = 5
N_EXPERTS = 32
TOP_K = 4
D_EXPERT = D_MODEL
SWIGLU_ALPHA = 1.702
SWIGLU_LIMIT = 7.0
PLE_DIM = 256
EPS = 1e-6
SCALE = HEAD_DIM ** -0.5
LANES = 128

F32 = jnp.float32
BF16 = jnp.bfloat16
VMEM_LIMIT = 56 * 1024 * 1024

IN_TM = 512
ATT_TQ = 256
OUT_TM = 256
MOE_TM = 512
MOE_TH = 512
FIN_TM = 256
GATHER_TM = 256


def _rms(x, g):
    return x * lax.rsqrt(jnp.mean(x * x, axis=-1, keepdims=True) + EPS) * g


def _params(sem, vmem=VMEM_LIMIT):
    return pltpu.CompilerParams(dimension_semantics=sem, vmem_limit_bytes=vmem)


def _in_proj_kernel(x_ref, g_ref, w_ref, lng_ref, lnb_ref,
                    q_ref, k_ref, v_ref, kb_ref, vb_ref, u_ref, vn_ref, a_scr):
    j = pl.program_id(1)

    @pl.when(j == 0)
    def _():
        a_scr[...] = _rms(x_ref[...], g_ref[...]).astype(BF16)

    z = jnp.dot(a_scr[...], w_ref[...], preferred_element_type=F32)

    @pl.when(j == 0)
    def _():
        q_ref[...] = z.astype(q_ref.dtype)

    @pl.when(j == 1)
    def _():
        k_ref[...] = z
        kb_ref[...] = z.astype(BF16)

    @pl.when(j == 2)
    def _():
        v_ref[...] = z
        vb_ref[...] = z.astype(BF16)

    @pl.when(j == 3)
    def _():
        u_ref[...] = jax.nn.gelu(z).astype(u_ref.dtype)

    @pl.when(j == 4)
    def _():
        gz = jax.nn.gelu(z)
        for g in range(GM_GROUPS):
            sl = slice(g * GM_CH, (g + 1) * GM_CH)
            blk = gz[:, sl]
            d = blk - jnp.mean(blk, axis=-1, keepdims=True)
            var = jnp.mean(d * d, axis=-1, keepdims=True)
            vn = d * lax.rsqrt(var + EPS) * lng_ref[:, sl] + lnb_ref[:, sl]
            vn_ref[:, sl] = vn.astype(vn_ref.dtype)


def _in_proj(x, g_mix, w_in_bf, ln_g, ln_b, q_dtype, vn_dtype):
    t = x.shape[0]
    tm = min(IN_TM, t)
    row = lambda i, j: (i, 0)
    sec = pl.BlockSpec((tm, SB_WIDTH), row)
    out_shape = [
        jax.ShapeDtypeStruct((t, SB_WIDTH), q_dtype),
        jax.ShapeDtypeStruct((t, SB_WIDTH), F32),
        jax.ShapeDtypeStruct((t, SB_WIDTH), F32),
        jax.ShapeDtypeStruct((t, SB_WIDTH), BF16),
        jax.ShapeDtypeStruct((t, SB_WIDTH), BF16),
        jax.ShapeDtypeStruct((t, GM_WIDTH), BF16),
        jax.ShapeDtypeStruct((t, GM_WIDTH), vn_dtype),
    ]
    return pl.pallas_call(
        _in_proj_kernel,
        out_shape=out_shape,
        grid=(t // tm, N_SECTIONS),
        in_specs=[
            pl.BlockSpec((tm, D_MODEL), row),
            pl.BlockSpec((1, D_MODEL), lambda i, j: (0, 0)),
            pl.BlockSpec((D_MODEL, SB_WIDTH), lambda i, j: (0, j)),
            pl.BlockSpec((1, GM_WIDTH), lambda i, j: (0, 0)),
            pl.BlockSpec((1, GM_WIDTH), lambda i, j: (0, 0)),
        ],
        out_specs=[sec] * 7,
        scratch_shapes=[pltpu.VMEM((tm, D_MODEL), BF16)],
        compiler_params=_params(("arbitrary", "arbitrary")),
        name="in_proj",
    )(x, g_mix, w_in_bf, ln_g, ln_b)


def _sb_tile(z, tri, carry, mask):
    l1p = jnp.log1p(jnp.exp(-jnp.abs(z)))
    log_beta = jnp.minimum(z, 0.0) - l1p
    log_1mb = jnp.minimum(-z, 0.0) - l1p
    if mask is not None:
        log_1mb = jnp.where(mask, log_1mb, 0.0)
    hi = log_1mb.astype(BF16)
    lo = (log_1mb - hi.astype(F32)).astype(BF16)
    excl = (jnp.dot(hi, tri, preferred_element_type=F32)
            + jnp.dot(lo, tri, preferred_element_type=F32) + carry)
    a = jnp.exp(log_beta + excl)
    if mask is not None:
        a = jnp.where(mask, a, 0.0)
    return a, jnp.sum(log_1mb, axis=1, keepdims=True)


def _strict_lower(n):
    r = lax.broadcasted_iota(jnp.int32, (n, n), 0)
    c = lax.broadcasted_iota(jnp.int32, (n, n), 1)
    return (r > c).astype(BF16), c < r


_NT = (((1,), (1,)), ((), ()))


def _sb_prompt_kernel(bias_ref, q_ref, k_ref, v_ref, o_ref, *, tq):
    h = pl.program_id(1)
    i = pl.program_id(2)
    bias = bias_ref[h]
    q = q_ref[...]
    tri, causal = _strict_lower(tq)

    def tile(j, carry, acc, mask):
        off = pl.multiple_of(j * tq, tq)
        k = k_ref[pl.ds(off, tq), :]
        v = v_ref[pl.ds(off, tq), :]
        z = lax.dot_general(q, k, _NT, preferred_element_type=F32) * SCALE + bias
        a, tot = _sb_tile(z, tri, carry, mask)
        acc = acc + jnp.dot(a.astype(BF16), v, preferred_element_type=F32)
        return carry + tot, acc

    carry, acc = tile(i, jnp.zeros((tq, 1), F32), jnp.zeros((tq, HEAD_DIM), F32), causal)

    def body(s, c):
        return tile(i - 1 - s, c[0], c[1], None)

    carry, acc = lax.fori_loop(0, i, body, (carry, acc))
    o_ref[...] = acc


def _sb_prompt(q_bf, k_bf, v_bf, sb_bias, batch, seq):
    tq = ATT_TQ
    nq = seq // tq
    k3 = k_bf.reshape(batch, seq, SB_WIDTH)
    v3 = v_bf.reshape(batch, seq, SB_WIDTH)
    qo = lambda b, h, i, bias: (b * nq + i, h)
    kv = lambda b, h, i, bias: (b, 0, h)
    grid_spec = pltpu.PrefetchScalarGridSpec(
        num_scalar_prefetch=1,
        grid=(batch, SB_HEADS, nq),
        in_specs=[
            pl.BlockSpec((tq, HEAD_DIM), qo),
            pl.BlockSpec((None, seq, HEAD_DIM), kv),
            pl.BlockSpec((None, seq, HEAD_DIM), kv),
        ],
        out_specs=pl.BlockSpec((tq, HEAD_DIM), qo),
    )
    return pl.pallas_call(
        functools.partial(_sb_prompt_kernel, tq=tq),
        out_shape=jax.ShapeDtypeStruct((batch * seq, SB_WIDTH), F32),
        grid_spec=grid_spec,
        compiler_params=_params(("arbitrary", "arbitrary", "arbitrary")),
        name="sb_prompt",
    )(sb_bias, q_bf, k3, v3)


def _sb_sample_kernel(pt_ref, bias_ref, q_ref, kn_ref, vn_ref, *rest, n_pages, dec_seq):
    k_refs = rest[:n_pages]
    v_refs = rest[n_pages:2 * n_pages]
    o_ref = rest[2 * n_pages]
    acc_ref = rest[2 * n_pages + 1]
    rows = SB_HEADS * dec_seq
    q = q_ref[...]
    qt = jnp.concatenate([q] * SB_HEADS, axis=0)
    rh = lax.broadcasted_iota(jnp.int32, (rows, SB_WIDTH), 0) // dec_seq
    ch = lax.broadcasted_iota(jnp.int32, (rows, SB_WIDTH), 1) // HEAD_DIM
    qbd = jnp.where(rh == ch, qt, 0.0).astype(BF16)
    rcol = lax.broadcasted_iota(jnp.int32, (rows, 1), 0) // dec_seq
    bias = jnp.zeros((rows, 1), F32)
    for h in range(SB_HEADS):
        bias = jnp.where(rcol == h, bias_ref[h], bias)
    tri, _ = _strict_lower(PAGE_SIZE)

    def tile(kb, vb, carry, mask):
        z = lax.dot_general(qbd, kb, _NT, preferred_element_type=F32) * SCALE + bias
        a, tot = _sb_tile(z, tri, carry, mask)
        return carry + tot, jnp.dot(a.astype(BF16), vb, preferred_element_type=F32)

    pad = jnp.zeros((PAGE_SIZE - dec_seq, SB_WIDTH), F32)
    kb = jnp.concatenate([kn_ref[...], pad], axis=0).astype(BF16)
    vb = jnp.concatenate([vn_ref[...], pad], axis=0).astype(BF16)
    qn = lax.broadcasted_iota(jnp.int32, (rows, PAGE_SIZE), 0) % dec_seq
    km = lax.broadcasted_iota(jnp.int32, (rows, PAGE_SIZE), 1)
    carry, part = tile(kb, vb, jnp.zeros((rows, 1), F32), km < qn)
    acc_ref[...] = part
    for p in reversed(range(n_pages)):
        carry, part = tile(k_refs[p][...].astype(BF16), v_refs[p][...].astype(BF16), carry, None)
        acc_ref[...] += part
    for h in range(SB_HEADS):
        sl = slice(h * HEAD_DIM, (h + 1) * HEAD_DIM)
        o_ref[:, sl] = acc_ref[h * dec_seq:(h + 1) * dec_seq, sl]


def _sb_sample(q, k_new, v_new, ck, cv, page_table, sb_bias, dec_seq):
    dec_b, n_pages = page_table.shape
    row = lambda b, pt, bias: (b, 0)
    page_specs = [
        pl.BlockSpec((None, PAGE_SIZE, SB_WIDTH),
                     lambda b, pt, bias, p=p: (pt[b * n_pages + p], 0, 0))
        for p in range(n_pages)
    ]
    grid_spec = pltpu.PrefetchScalarGridSpec(
        num_scalar_prefetch=2,
        grid=(dec_b,),
        in_specs=[pl.BlockSpec((dec_seq, SB_WIDTH), row)] * 3 + page_specs + page_specs,
        out_specs=pl.BlockSpec((dec_seq, SB_WIDTH), row),
        scratch_shapes=[pltpu.VMEM((SB_HEADS * dec_seq, SB_WIDTH), F32)],
    )
    return pl.pallas_call(
        functools.partial(_sb_sample_kernel, n_pages=n_pages, dec_seq=dec_seq),
        out_shape=jax.ShapeDtypeStruct((dec_b * dec_seq, SB_WIDTH), F32),
        grid_spec=grid_spec,
        compiler_params=_params(("arbitrary",)),
        name="sb_sample",
    )(page_table.reshape(-1), sb_bias, q, k_new, v_new, *([ck] * n_pages), *([cv] * n_pages))


def _out_proj_kernel(x_ref, sbo_ref, u_ref, vn_ref, wm_ref, bm_ref, gsb_ref, ggm_ref, wout_ref,
                     gffn_ref, wrh_ref, wrl_ref, br_ref,
                     h_ref, f_ref, idx_ref, gate_ref, gm_scr, mixed_scr, *, period):
    tm = x_ref.shape[0]
    r = lax.broadcasted_iota(jnp.int32, (CHUNK, CHUNK), 0)
    c = lax.broadcasted_iota(jnp.int32, (CHUNK, CHUNK), 1)
    keep = (r // period == c // period) & (c % period <= r % period)
    for g in range(GM_GROUPS):
        sl = slice(g * GM_CH, (g + 1) * GM_CH)
        wm = jnp.where(keep, wm_ref[g], 0.0).astype(BF16)
        for n in range(tm // CHUNK):
            rs = slice(n * CHUNK, (n + 1) * CHUNK)
            mix = jnp.dot(wm, vn_ref[rs, sl].astype(BF16), preferred_element_type=F32) + bm_ref[g]
            gm_scr[rs, sl] = u_ref[rs, sl].astype(F32) * mix
    mixed_scr[:, :SB_WIDTH] = _rms(sbo_ref[...], gsb_ref[...]).astype(BF16)
    mixed_scr[:, SB_WIDTH:] = _rms(gm_scr[...], ggm_ref[...]).astype(BF16)
    h = x_ref[...] + jnp.dot(mixed_scr[...], wout_ref[...], preferred_element_type=F32)
    h_ref[...] = h
    f = _rms(h, gffn_ref[...])
    f_ref[...] = f
    f_hi = f.astype(BF16)
    f_lo = (f - f_hi.astype(F32)).astype(BF16)
    logits = (jnp.dot(f_hi, wrh_ref[...], preferred_element_type=F32)
              + jnp.dot(f_lo, wrh_ref[...], preferred_element_type=F32)
              + jnp.dot(f_hi, wrl_ref[...], preferred_element_type=F32)) + br_ref[...]
    lane = lax.broadcasted_iota(jnp.int32, (tm, LANES), 1)
    vals, idxs = [], []
    for _ in range(TOP_K):
        m = jnp.max(logits, axis=1, keepdims=True)
        am = jnp.min(jnp.where(logits == m, lane, LANES), axis=1, keepdims=True)
        vals.append(m)
        idxs.append(am)
        logits = jnp.where(lane == am, -jnp.inf, logits)
    es = [jnp.exp(v - vals[0]) for v in vals]
    denom = es[0] + es[1] + es[2] + es[3]
    idx_out = jnp.zeros((tm, LANES), jnp.int32)
    gate_out = jnp.zeros((tm, LANES), F32)
    for k in range(TOP_K):
        idx_out = jnp.where(lane == k, idxs[k], idx_out)
        gate_out = jnp.where(lane == k, es[k] / denom, gate_out)
    idx_ref[...] = idx_out
    gate_ref[...] = gate_out


def _out_proj(x, sb_o, u, vn, wm, bm, g_sb, g_gm, w_out_bf, g_ffn, wr_hi, wr_lo, b_r, period):
    t = x.shape[0]
    tm = OUT_TM
    row = lambda i: (i, 0)
    const2 = lambda i: (0, 0)
    const3 = lambda i: (0, 0, 0)
    return pl.pallas_call(
        functools.partial(_out_proj_kernel, period=period),
        out_shape=[
            jax.ShapeDtypeStruct((t, D_MODEL), F32),
            jax.ShapeDtypeStruct((t, D_MODEL), F32),
            jax.ShapeDtypeStruct((t, LANES), jnp.int32),
            jax.ShapeDtypeStruct((t, LANES), F32),
        ],
        grid=(t // tm,),
        in_specs=[
            pl.BlockSpec((tm, D_MODEL), row),
            pl.BlockSpec((tm, SB_WIDTH), row),
            pl.BlockSpec((tm, GM_WIDTH), row),
            pl.BlockSpec((tm, GM_WIDTH), row),
            pl.BlockSpec((GM_GROUPS, CHUNK, CHUNK), const3),
            pl.BlockSpec((GM_GROUPS, CHUNK, 1), const3),
            pl.BlockSpec((1, SB_WIDTH), const2),
            pl.BlockSpec((1, GM_WIDTH), const2),
            pl.BlockSpec((D_MODEL, D_MODEL), const2),
            pl.BlockSpec((1, D_MODEL), const2),
            pl.BlockSpec((D_MODEL, LANES), const2),
            pl.BlockSpec((D_MODEL, LANES), const2),
            pl.BlockSpec((1, LANES), const2),
        ],
        out_specs=[
            pl.BlockSpec((tm, D_MODEL), row),
            pl.BlockSpec((tm, D_MODEL), row),
            pl.BlockSpec((tm, LANES), row),
            pl.BlockSpec((tm, LANES), row),
        ],
        scratch_shapes=[pltpu.VMEM((tm, GM_WIDTH), F32), pltpu.VMEM((tm, D_MODEL), BF16)],
        compiler_params=_params(("arbitrary",)),
        name="out_proj",
    )(x, sb_o, u, vn, wm, bm, g_sb, g_gm, w_out_bf, g_ffn, wr_hi, wr_lo, b_r)


def _gather_kernel(idx_ref, table_ref, o_ref, sem, *, tm):
    def row_copy(r, src_row):
        return pltpu.make_async_copy(table_ref.at[pl.ds(src_row, 1), :], o_ref.at[pl.ds(r, 1), :], sem)

    def start(r, _):
        row_copy(r, idx_ref[0, 0, r]).start()
        return 0

    def wait(r, _):
        row_copy(r, 0).wait()
        return 0

    lax.fori_loop(0, tm, start, 0, unroll=8)
    lax.fori_loop(0, tm, wait, 0, unroll=8)


def _gather_rows(table, idx):
    n = idx.shape[0]
    tm = GATHER_TM
    width = table.shape[1]
    return pl.pallas_call(
        functools.partial(_gather_kernel, tm=tm),
        out_shape=jax.ShapeDtypeStruct((n, width), table.dtype),
        grid=(n // tm,),
        in_specs=[
            pl.BlockSpec((1, 1, tm), lambda i: (i, 0, 0), memory_space=pltpu.SMEM),
            pl.BlockSpec(memory_space=pl.ANY),
        ],
        out_specs=pl.BlockSpec((tm, width), lambda i: (i, 0)),
        scratch_shapes=[pltpu.SemaphoreType.DMA],
        compiler_params=_params(("arbitrary",)),
        name="gather_rows",
    )(idx.reshape(n // tm, 1, tm), table)


def _moe_kernel(te_ref, nu_ref, x_ref, w1g_ref, w1l_ref, b1g_ref, b1l_ref, w2_ref, b2_ref, o_ref):
    t = pl.program_id(0)
    c = pl.program_id(1)
    used = t < nu_ref[0]

    @pl.when(used)
    def _():
        x = x_ref[...].astype(BF16)
        hg = jnp.dot(x, w1g_ref[0].astype(BF16), preferred_element_type=F32) + b1g_ref[0]
        hl = jnp.dot(x, w1l_ref[0].astype(BF16), preferred_element_type=F32) + b1l_ref[0]
        glu = jnp.minimum(hg, SWIGLU_LIMIT)
        lin = jnp.clip(hl, -SWIGLU_LIMIT, SWIGLU_LIMIT)
        act = glu * jax.nn.sigmoid(SWIGLU_ALPHA * glu) * (lin + 1.0)
        part = jnp.dot(act.astype(BF16), w2_ref[0].astype(BF16), preferred_element_type=F32)

        @pl.when(c == 0)
        def _():
            o_ref[...] = part + b2_ref[0]

        @pl.when(c > 0)
        def _():
            o_ref[...] += part

    @pl.when(jnp.logical_not(used) & (c == 0))
    def _():
        o_ref[...] = jnp.zeros_like(o_ref)


def _moe(xs, tile_exp, n_used, w1, b1, w2, b2):
    n_rows = xs.shape[0]
    tm, th = MOE_TM, MOE_TH
    nc = D_EXPERT // th
    n_tiles = n_rows // tm

    def chunk(t, c, te, nu):
        return jnp.where(t < nu[0], c, nc - 1)

    grid_spec = pltpu.PrefetchScalarGridSpec(
        num_scalar_prefetch=2,
        grid=(n_tiles, nc),
        in_specs=[
            pl.BlockSpec((tm, D_MODEL), lambda t, c, te, nu: (t, 0)),
            pl.BlockSpec((1, D_MODEL, th), lambda t, c, te, nu: (te[t], 0, chunk(t, c, te, nu))),
            pl.BlockSpec((1, D_MODEL, th), lambda t, c, te, nu: (te[t], 0, nc + chunk(t, c, te, nu))),
            pl.BlockSpec((1, 1, th), lambda t, c, te, nu: (te[t], 0, chunk(t, c, te, nu))),
            pl.BlockSpec((1, 1, th), lambda t, c, te, nu: (te[t], 0, nc + chunk(t, c, te, nu))),
            pl.BlockSpec((1, th, D_MODEL), lambda t, c, te, nu: (te[t], chunk(t, c, te, nu), 0)),
            pl.BlockSpec((1, 1, D_MODEL), lambda t, c, te, nu: (te[t], 0, 0)),
        ],
        out_specs=pl.BlockSpec((tm, D_MODEL), lambda t, c, te, nu: (t, 0)),
    )
    return pl.pallas_call(
        _moe_kernel,
        out_shape=jax.ShapeDtypeStruct((n_rows, D_MODEL), F32),
        grid_spec=grid_spec,
        compiler_params=_params(("arbitrary", "arbitrary")),
        name="moe_experts",
    )(tile_exp, n_used, xs, w1, w1, b1, b1, w2, b2)


def _final_kernel(h_ref, y0_ref, y1_ref, y2_ref, y3_ref, gate_ref, p_ref, gple_ref, wg_ref, wp_ref,
                  gfin_ref, o_ref):
    gates = gate_ref[...]
    h = h_ref[...]
    for k, y_ref in enumerate((y0_ref, y1_ref, y2_ref, y3_ref)):
        h = h + y_ref[...] * gates[:, k:k + 1]
    a = _rms(h, gple_ref[...]).astype(BF16)
    gate = jax.nn.sigmoid(jnp.dot(a, wg_ref[...], preferred_element_type=F32))
    proj = jnp.dot(p_ref[...].astype(BF16), wp_ref[...], preferred_element_type=F32)
    o_ref[...] = _rms(h + proj * gate, gfin_ref[...])


def _final(h, yg, gates, p, g_ple, w_gate_bf, w_proj_bf, g_final):
    t = h.shape[0]
    tm = FIN_TM
    nt = t // tm
    row = lambda i: (i, 0)
    const2 = lambda i: (0, 0)
    y_specs = [pl.BlockSpec((tm, D_MODEL), lambda i, k=k: (k * nt + i, 0)) for k in range(TOP_K)]
    return pl.pallas_call(
        _final_kernel,
        out_shape=jax.ShapeDtypeStruct((t, D_MODEL), F32),
        grid=(nt,),
        in_specs=[pl.BlockSpec((tm, D_MODEL), row)] + y_specs + [
            pl.BlockSpec((tm, LANES), row),
            pl.BlockSpec((tm, PLE_DIM), row),
            pl.BlockSpec((1, D_MODEL), const2),
            pl.BlockSpec((D_MODEL, D_MODEL), const2),
            pl.BlockSpec((PLE_DIM, D_MODEL), const2),
            pl.BlockSpec((1, D_MODEL), const2),
        ],
        out_specs=pl.BlockSpec((tm, D_MODEL), row),
        compiler_params=_params(("arbitrary",)),
        name="final",
    )(h, yg, yg, yg, yg, gates, p, g_ple, w_gate_bf, w_proj_bf, g_final)


def _route(top_idx, tm):
    t = top_idx.shape[0]
    e_flat = top_idx.reshape(-1)
    tok_flat = jnp.repeat(jnp.arange(t, dtype=jnp.int32), TOP_K)
    onehot = (e_flat[:, None] == jnp.arange(N_EXPERTS, dtype=jnp.int32)[None, :]).astype(jnp.int32)
    csum = jnp.cumsum(onehot, axis=0)
    rank = jnp.sum((csum - onehot) * onehot, axis=1)
    counts = csum[-1]
    padded = (counts + tm - 1) // tm * tm
    pend = jnp.cumsum(padded)
    pstart = pend - padded
    pos = jnp.sum(onehot * pstart[None, :], axis=1) + rank
    n_tiles = -(-(t * TOP_K) // tm) + N_EXPERTS
    row_tok = jnp.zeros((n_tiles * tm,), jnp.int32).at[pos].set(tok_flat)
    tile_start = jnp.arange(n_tiles, dtype=jnp.int32) * tm
    tile_exp = jnp.clip(jnp.searchsorted(pend, tile_start, side='right'), 0, N_EXPERTS - 1).astype(jnp.int32)
    n_used = (pend[-1] // tm).astype(jnp.int32).reshape(1)
    last_exp = tile_exp[jnp.maximum(n_used[0] - 1, 0)]
    tile_exp = jnp.where(tile_start // tm < n_used[0], tile_exp, last_exp)
    pos_slot_major = pos.reshape(t, TOP_K).T.reshape(-1)
    return row_tok, pos_slot_major, tile_exp, n_used


def kernel(x_prompt, x_sample, cache_k, cache_v, page_table, p_prompt, p_sample, g_mix, w_in, sb_bias,
           gm_ln_g, gm_ln_b, gm_w_s, gm_b_s, g_sb_out, g_gm_out, w_out, g_ffn, w_router, b_router,
           w_exp1, b_exp1, w_exp2, b_exp2, g_ple, w_ple_gate, w_ple_proj, g_final):
    assert g_mix.shape[0] == 1, "single-layer step only"
    batch, seq, _ = x_prompt.shape
    dec_b, dec_seq, _ = x_sample.shape
    n_phys = cache_k.shape[1]
    tp, ts = batch * seq, dec_b * dec_seq
    hp = x_prompt.reshape(tp, D_MODEL)
    hs = x_sample.reshape(ts, D_MODEL)

    w_in_bf = w_in.reshape(D_MODEL, N_SECTIONS * SB_WIDTH).astype(BF16)
    w_out_bf = w_out.reshape(D_MODEL, D_MODEL).astype(BF16)
    w_gate_bf = w_ple_gate.reshape(D_MODEL, D_MODEL).astype(BF16)
    w_proj_bf = w_ple_proj.reshape(PLE_DIM, D_MODEL).astype(BF16)
    gmix = g_mix.reshape(1, D_MODEL)
    ln_g = gm_ln_g.reshape(1, GM_WIDTH)
    ln_b = gm_ln_b.reshape(1, GM_WIDTH)
    g_sb = g_sb_out.reshape(1, SB_WIDTH)
    g_gm = g_gm_out.reshape(1, GM_WIDTH)
    gffn = g_ffn.reshape(1, D_MODEL)
    gple = g_ple.reshape(1, D_MODEL)
    gfin = g_final.reshape(1, D_MODEL)
    bias = sb_bias.reshape(SB_HEADS)
    wr = jnp.pad(w_router.reshape(D_MODEL, N_EXPERTS), ((0, 0), (0, LANES - N_EXPERTS)))
    wr_hi = wr.astype(BF16)
    wr_lo = (wr - wr_hi.astype(F32)).astype(BF16)
    b_r = jnp.pad(b_router.reshape(N_EXPERTS), (0, LANES - N_EXPERTS),
                  constant_values=-jnp.inf).reshape(1, LANES)
    w_s = gm_w_s.reshape(GM_GROUPS, CHUNK, CHUNK)
    b_s = gm_b_s.reshape(GM_GROUPS, CHUNK)
    reps = CHUNK // dec_seq
    wm_s = jnp.tile(w_s[:, :dec_seq, :dec_seq], (1, reps, reps))
    bm_s = jnp.tile(b_s[:, :dec_seq], (1, reps)).reshape(GM_GROUPS, CHUNK, 1)
    ck = cache_k.reshape(n_phys, PAGE_SIZE, SB_WIDTH)
    cv = cache_v.reshape(n_phys, PAGE_SIZE, SB_WIDTH)

    qp, kp, vp, kpb, vpb, up, vnp = _in_proj(hp, gmix, w_in_bf, ln_g, ln_b, BF16, BF16)
    sbo_p = _sb_prompt(qp, kpb, vpb, bias, batch, seq)
    h1p, fp, idxp, gatep = _out_proj(hp, sbo_p, up, vnp, w_s, b_s.reshape(GM_GROUPS, CHUNK, 1), g_sb, g_gm,
                                     w_out_bf, gffn, wr_hi, wr_lo, b_r, CHUNK)
    qs, ksn, vsn, _, _, us, vns = _in_proj(hs, gmix, w_in_bf, ln_g, ln_b, F32, F32)
    sbo_s = _sb_sample(qs, ksn, vsn, ck, cv, page_table, bias, dec_seq)
    h1s, fs, idxs, gates_s = _out_proj(hs, sbo_s, us, vns, wm_s, bm_s, g_sb, g_gm, w_out_bf, gffn,
                                       wr_hi, wr_lo, b_r, dec_seq)

    f_all = jnp.concatenate([fp, fs], axis=0)
    h1 = jnp.concatenate([h1p, h1s], axis=0)
    gates = jnp.concatenate([gatep, gates_s], axis=0)
    top_idx = jnp.concatenate([idxp, idxs], axis=0)[:, :TOP_K]
    row_tok, pos, tile_exp, n_used = _route(top_idx, MOE_TM)
    xs = _gather_rows(f_all, row_tok)
    yb = _moe(xs, tile_exp, n_used,
              w_exp1.reshape(N_EXPERTS, D_MODEL, 2 * D_EXPERT), b_exp1.reshape(N_EXPERTS, 1, 2 * D_EXPERT),
              w_exp2.reshape(N_EXPERTS, D_EXPERT, D_MODEL), b_exp2.reshape(N_EXPERTS, 1, D_MODEL))
    yg = _gather_rows(yb, pos)
    p_all = jnp.concatenate([p_prompt.reshape(tp, PLE_DIM), p_sample.reshape(ts, PLE_DIM)], axis=0)
    y = _final(h1, yg, gates, p_all, gple, w_gate_bf, w_proj_bf, gfin)

    kv_p = (1, batch, seq, SB_HEADS, HEAD_DIM)
    kv_s = (1, dec_b, dec_seq, SB_HEADS, HEAD_DIM)
    return (y[:tp].reshape(batch, seq, D_MODEL), y[tp:].reshape(dec_b, dec_seq, D_MODEL),
            kp.reshape(kv_p), vp.reshape(kv_p), ksn.reshape(kv_s), vsn.reshape(kv_s),
            vns.reshape(1, dec_b, dec_seq, GM_GROUPS, GM_CH))
```

```python
import functools

import jax
import jax.numpy as jnp
from jax import lax
from jax.experimental import pallas as pl
from jax.experimental.pallas import tpu as pltpu

D_MODEL = 2048
SB_HEADS = 8
HEAD_DIM = 128
SB_WIDTH = SB_HEADS * HEAD_DIM
GM_GROUPS = 8
GM_CH = 128
GM_WIDTH = GM_GROUPS * GM_CH
CHUNK = 128
PAGE_SIZE = 128
N_SECTIONS = 5
N_EXPERTS = 32
TOP_K = 4
D_EXPERT = D_MODEL
SWIGLU_ALPHA = 1.702
SWIGLU_LIMIT = 7.0
PLE_DIM = 256
EPS = 1e-6
SCALE = HEAD_DIM ** -0.5
LANES = 128

F32 = jnp.float32
BF16 = jnp.bfloat16
VMEM_LIMIT = 56 * 1024 * 1024

IN_TM = 512
ATT_TQ = 256
ATT_HEADS = 4
SAMPLE_PAGES = 8
SAMPLE_POS = 32
OUT_TM = 256
MOE_R = 1280
MOE_SUB = 256
MOE_TH = 256
FIN_TM = 256
PACKED = D_MODEL // 2


def _rms(x, g):
    return x * lax.rsqrt(jnp.mean(x * x, axis=-1, keepdims=True) + EPS) * g


def _params(sem, vmem=VMEM_LIMIT):
    return pltpu.CompilerParams(dimension_semantics=sem, vmem_limit_bytes=vmem)


def _in_proj_kernel(x_ref, g_ref, w_ref, lng_ref, lnb_ref,
                    q_ref, k_ref, v_ref, kb_ref, vb_ref, u_ref, vn_ref, a_scr):
    j = pl.program_id(1)

    @pl.when(j == 0)
    def _():
        a_scr[...] = _rms(x_ref[...], g_ref[...]).astype(BF16)

    z = jnp.dot(a_scr[...], w_ref[...], preferred_element_type=F32)

    @pl.when(j == 0)
    def _():
        q_ref[...] = z.astype(q_ref.dtype)

    @pl.when(j == 1)
    def _():
        k_ref[...] = z
        kb_ref[...] = z.astype(BF16)

    @pl.when(j == 2)
    def _():
        v_ref[...] = z
        vb_ref[...] = z.astype(BF16)

    @pl.when(j == 3)
    def _():
        u_ref[...] = jax.nn.gelu(z).astype(u_ref.dtype)

    @pl.when(j == 4)
    def _():
        gz = jax.nn.gelu(z)
        for g in range(GM_GROUPS):
            sl = slice(g * GM_CH, (g + 1) * GM_CH)
            blk = gz[:, sl]
            d = blk - jnp.mean(blk, axis=-1, keepdims=True)
            var = jnp.mean(d * d, axis=-1, keepdims=True)
            vn = d * lax.rsqrt(var + EPS) * lng_ref[:, sl] + lnb_ref[:, sl]
            vn_ref[:, sl] = vn.astype(vn_ref.dtype)


def _in_proj(x, g_mix, w_in_bf, ln_g, ln_b, q_dtype, vn_dtype):
    t = x.shape[0]
    tm = min(IN_TM, t)
    row = lambda i, j: (i, 0)
    sec = pl.BlockSpec((tm, SB_WIDTH), row)
    out_shape = [
        jax.ShapeDtypeStruct((t, SB_WIDTH), q_dtype),
        jax.ShapeDtypeStruct((t, SB_WIDTH), F32),
        jax.ShapeDtypeStruct((t, SB_WIDTH), F32),
        jax.ShapeDtypeStruct((t, SB_WIDTH), BF16),
        jax.ShapeDtypeStruct((t, SB_WIDTH), BF16),
        jax.ShapeDtypeStruct((t, GM_WIDTH), BF16),
        jax.ShapeDtypeStruct((t, GM_WIDTH), vn_dtype),
    ]
    return pl.pallas_call(
        _in_proj_kernel,
        out_shape=out_shape,
        grid=(t // tm, N_SECTIONS),
        in_specs=[
            pl.BlockSpec((tm, D_MODEL), row),
            pl.BlockSpec((1, D_MODEL), lambda i, j: (0, 0)),
            pl.BlockSpec((D_MODEL, SB_WIDTH), lambda i, j: (0, j)),
            pl.BlockSpec((1, GM_WIDTH), lambda i, j: (0, 0)),
            pl.BlockSpec((1, GM_WIDTH), lambda i, j: (0, 0)),
        ],
        out_specs=[sec] * 7,
        scratch_shapes=[pltpu.VMEM((tm, D_MODEL), BF16)],
        compiler_params=_params(("arbitrary", "arbitrary")),
        name="in_proj",
    )(x, g_mix, w_in_bf, ln_g, ln_b)


def _sb_logs(z, mask):
    l1p = jnp.log(1.0 + jnp.exp(-jnp.abs(z)))
    log_beta = jnp.minimum(z, 0.0) - l1p
    log_1mb = jnp.minimum(-z, 0.0) - l1p
    if mask is not None:
        log_1mb = jnp.where(mask, log_1mb, 0.0)
    hi = log_1mb.astype(BF16)
    lo = (log_1mb - hi.astype(F32)).astype(BF16)
    return log_beta, hi, lo, jnp.sum(log_1mb, axis=1, keepdims=True)


def _sb_later(hi, lo, tri):
    return jnp.dot(hi, tri, preferred_element_type=F32) + jnp.dot(lo, tri, preferred_element_type=F32)


def _sb_weights(log_beta, later, carry, mask):
    a = jnp.exp(log_beta + later + carry)
    if mask is not None:
        a = jnp.where(mask, a, 0.0)
    return a.astype(BF16)


def _strict_lower(n):
    r = lax.broadcasted_iota(jnp.int32, (n, n), 0)
    c = lax.broadcasted_iota(jnp.int32, (n, n), 1)
    return (r > c).astype(BF16), c < r


_NT = (((1,), (1,)), ((), ()))


def _sb_prompt_kernel(bias_ref, q_ref, k_ref, v_ref, o_ref, *, tq, nh):
    hg = pl.program_id(1)
    i = pl.program_id(2)
    tri, causal = _strict_lower(tq)

    def tile(j, state, mask):
        rows = pl.ds(pl.multiple_of(j * tq, tq), tq)
        heads = [slice(hh * HEAD_DIM, (hh + 1) * HEAD_DIM) for hh in range(nh)]
        logs = []
        for hh, sl in enumerate(heads):
            z = lax.dot_general(q_ref[:, sl], k_ref[rows, sl], _NT, preferred_element_type=F32) * SCALE
            logs.append(_sb_logs(z + bias_ref[hg * nh + hh], mask))
        later = [_sb_later(hi, lo, tri) for _, hi, lo, _ in logs]
        new = []
        for hh, sl in enumerate(heads):
            carry, acc = state[hh]
            a = _sb_weights(logs[hh][0], later[hh], carry, mask)
            new.append((carry + logs[hh][3], acc + jnp.dot(a, v_ref[rows, sl], preferred_element_type=F32)))
        return tuple(new)

    zero = (jnp.zeros((tq, 1), F32), jnp.zeros((tq, HEAD_DIM), F32))
    state = tile(i, (zero,) * nh, causal)
    state = lax.fori_loop(0, i, lambda s, st: tile(i - 1 - s, st, None), state)
    for hh in range(nh):
        o_ref[:, hh * HEAD_DIM:(hh + 1) * HEAD_DIM] = state[hh][1]


def _sb_prompt(q_bf, k_bf, v_bf, sb_bias, batch, seq):
    tq, nh = ATT_TQ, ATT_HEADS
    nq = seq // tq
    k3 = k_bf.reshape(batch, seq, SB_WIDTH)
    v3 = v_bf.reshape(batch, seq, SB_WIDTH)
    qo = lambda b, h, i, bias: (b * nq + i, h)
    kv = lambda b, h, i, bias: (b, 0, h)
    grid_spec = pltpu.PrefetchScalarGridSpec(
        num_scalar_prefetch=1,
        grid=(batch, SB_HEADS // nh, nq),
        in_specs=[
            pl.BlockSpec((tq, nh * HEAD_DIM), qo),
            pl.BlockSpec((None, seq, nh * HEAD_DIM), kv),
            pl.BlockSpec((None, seq, nh * HEAD_DIM), kv),
        ],
        out_specs=pl.BlockSpec((tq, nh * HEAD_DIM), qo),
    )
    return pl.pallas_call(
        functools.partial(_sb_prompt_kernel, tq=tq, nh=nh),
        out_shape=jax.ShapeDtypeStruct((batch * seq, SB_WIDTH), F32),
        grid_spec=grid_spec,
        compiler_params=_params(("arbitrary", "arbitrary", "arbitrary")),
        name="sb_prompt",
    )(sb_bias, q_bf, k3, v3)


def _sb_sample_kernel(pt_ref, bias_ref, q_ref, kn_ref, vn_ref, *rest, pages, dec_seq):
    k_refs = rest[:pages]
    v_refs = rest[pages:2 * pages]
    o_ref, carry_scr, acc_scr = rest[2 * pages:]
    j = pl.program_id(1)
    rows = SB_HEADS * dec_seq
    lanes = SAMPLE_POS * SB_HEADS

    def by_head(ref):
        return jnp.concatenate([ref[:, h * HEAD_DIM:(h + 1) * HEAD_DIM] for h in range(SB_HEADS)], axis=0)

    q = by_head(q_ref).astype(BF16)
    rhead = lax.broadcasted_iota(jnp.int32, (rows, 1), 0) // dec_seq
    bias = jnp.zeros((rows, 1), F32)
    for h in range(SB_HEADS):
        bias = jnp.where(rhead == h, bias_ref[h], bias)
    tri, _ = _strict_lower(lanes)

    def logits(kb):
        return lax.dot_general(q, kb, _NT, preferred_element_type=F32) * SCALE + bias

    @pl.when(j == 0)
    def _():
        pad = jnp.zeros((LANES - rows, HEAD_DIM), F32)
        kb = jnp.concatenate([by_head(kn_ref), pad], axis=0).astype(BF16)
        vb = jnp.concatenate([by_head(vn_ref), pad], axis=0).astype(BF16)
        r = lax.broadcasted_iota(jnp.int32, (rows, LANES), 0)
        l = lax.broadcasted_iota(jnp.int32, (rows, LANES), 1)
        mask = (l // dec_seq == r // dec_seq) & (l % dec_seq < r % dec_seq)
        log_beta, hi, lo, tot = _sb_logs(logits(kb), mask)
        a = _sb_weights(log_beta, _sb_later(hi, lo, tri[:LANES, :LANES]), 0.0, mask)
        carry_scr[...] = tot
        acc_scr[...] = jnp.dot(a, vb, preferred_element_type=F32)

    r = lax.broadcasted_iota(jnp.int32, (rows, lanes), 0)
    l = lax.broadcasted_iota(jnp.int32, (rows, lanes), 1)
    own_head = l % SB_HEADS == r // dec_seq
    tiles = PAGE_SIZE // SAMPLE_POS
    page_rows = PAGE_SIZE * SB_HEADS

    def stage_logs(p):
        z = logits(k_refs[p][...].reshape(page_rows, HEAD_DIM).astype(BF16))
        return [_sb_logs(z[:, t * lanes:(t + 1) * lanes], own_head) for t in range(tiles)]

    def stage_later(logs):
        return [_sb_later(hi, lo, tri) for _, hi, lo, _ in logs]

    def stage_weights(p, logs, later, carry, acc):
        a = [None] * tiles
        for t in reversed(range(tiles)):
            a[t] = _sb_weights(logs[t][0], later[t], carry, own_head)
            carry = carry + logs[t][3]
        vb = v_refs[p][...].reshape(page_rows, HEAD_DIM).astype(BF16)
        return carry, acc + jnp.dot(jnp.concatenate(a, axis=1), vb, preferred_element_type=F32)

    order = list(reversed(range(pages)))
    carry = carry_scr[...]
    acc = acc_scr[...]
    logs = {0: stage_logs(order[0]), 1: stage_logs(order[1])}
    later = {0: stage_later(logs[0])}
    for u in range(pages):
        if u + 2 < pages:
            logs[u + 2] = stage_logs(order[u + 2])
        if u + 1 < pages:
            later[u + 1] = stage_later(logs[u + 1])
        carry, acc = stage_weights(order[u], logs.pop(u), later.pop(u), carry, acc)
    carry_scr[...] = carry
    acc_scr[...] = acc

    @pl.when(j == pl.num_programs(1) - 1)
    def _():
        for h in range(SB_HEADS):
            o_ref[:, h * HEAD_DIM:(h + 1) * HEAD_DIM] = acc[h * dec_seq:(h + 1) * dec_seq, :]


def _sb_sample(q, k_new, v_new, ck, cv, page_table, sb_bias, dec_seq):
    dec_b, n_pages = page_table.shape
    pages = SAMPLE_PAGES
    nj = n_pages // pages
    row = lambda b, j, pt, bias: (b, 0)
    page_specs = [
        pl.BlockSpec((None, PAGE_SIZE, SB_HEADS, HEAD_DIM),
                     lambda b, j, pt, bias, p=p: (pt[b * n_pages + (nj - 1 - j) * pages + p], 0, 0, 0))
        for p in range(pages)
    ]
    rows = SB_HEADS * dec_seq
    grid_spec = pltpu.PrefetchScalarGridSpec(
        num_scalar_prefetch=2,
        grid=(dec_b, nj),
        in_specs=[pl.BlockSpec((dec_seq, SB_WIDTH), row)] * 3 + page_specs + page_specs,
        out_specs=pl.BlockSpec((dec_seq, SB_WIDTH), row),
        scratch_shapes=[pltpu.VMEM((rows, 1), F32), pltpu.VMEM((rows, HEAD_DIM), F32)],
    )
    return pl.pallas_call(
        functools.partial(_sb_sample_kernel, pages=pages, dec_seq=dec_seq),
        out_shape=jax.ShapeDtypeStruct((dec_b * dec_seq, SB_WIDTH), F32),
        grid_spec=grid_spec,
        compiler_params=_params(("arbitrary", "arbitrary")),
        name="sb_sample",
    )(page_table.reshape(-1), sb_bias, q, k_new, v_new, *([ck] * pages), *([cv] * pages))


def _out_proj_kernel(x_ref, sbo_ref, u_ref, vn_ref, wm_ref, bm_ref, gsb_ref, ggm_ref, wout_ref,
                     gffn_ref, wrh_ref, wrl_ref, br_ref,
                     h_ref, f_ref, idx_ref, gate_ref, gm_scr, mixed_scr, *, period):
    tm = x_ref.shape[0]
    r = lax.broadcasted_iota(jnp.int32, (CHUNK, CHUNK), 0)
    c = lax.broadcasted_iota(jnp.int32, (CHUNK, CHUNK), 1)
    keep = (r // period == c // period) & (c % period <= r % period)
    for g in range(GM_GROUPS):
        sl = slice(g * GM_CH, (g + 1) * GM_CH)
        wm = jnp.where(keep, wm_ref[g], 0.0).astype(BF16)
        for n in range(tm // CHUNK):
            rs = slice(n * CHUNK, (n + 1) * CHUNK)
            mix = jnp.dot(wm, vn_ref[rs, sl].astype(BF16), preferred_element_type=F32) + bm_ref[g]
            gm_scr[rs, sl] = u_ref[rs, sl].astype(F32) * mix
    mixed_scr[:, :SB_WIDTH] = _rms(sbo_ref[...], gsb_ref[...]).astype(BF16)
    mixed_scr[:, SB_WIDTH:] = _rms(gm_scr[...], ggm_ref[...]).astype(BF16)
    h = x_ref[...] + jnp.dot(mixed_scr[...], wout_ref[...], preferred_element_type=F32)
    h_ref[...] = h
    f = _rms(h, gffn_ref[...])
    f_hi = f.astype(BF16)
    bits = lax.bitcast_convert_type(f_hi.astype(F32), jnp.uint32)
    f_ref[...] = (bits[:, :PACKED] >> 16) | bits[:, PACKED:]
    f_lo = (f - f_hi.astype(F32)).astype(BF16)
    logits = (jnp.dot(f_hi, wrh_ref[...], preferred_element_type=F32)
              + jnp.dot(f_lo, wrh_ref[...], preferred_element_type=F32)
              + jnp.dot(f_hi, wrl_ref[...], preferred_element_type=F32)) + br_ref[...]
    lane = lax.broadcasted_iota(jnp.int32, (tm, LANES), 1)
    vals, idxs = [], []
    for _ in range(TOP_K):
        m = jnp.max(logits, axis=1, keepdims=True)
        am = jnp.min(jnp.where(logits == m, lane, LANES), axis=1, keepdims=True)
        vals.append(m)
        idxs.append(am)
        logits = jnp.where(lane == am, -jnp.inf, logits)
    es = [jnp.exp(v - vals[0]) for v in vals]
    denom = es[0] + es[1] + es[2] + es[3]
    idx_out = jnp.zeros((tm, LANES), jnp.int32)
    gate_out = jnp.zeros((tm, LANES), F32)
    for k in range(TOP_K):
        idx_out = jnp.where(lane == k, idxs[k], idx_out)
        gate_out = jnp.where(lane == k, es[k] / denom, gate_out)
    idx_ref[...] = idx_out
    gate_ref[...] = gate_out


def _out_proj(x, sb_o, u, vn, wm, bm, g_sb, g_gm, w_out_bf, g_ffn, wr_hi, wr_lo, b_r, period):
    t = x.shape[0]
    tm = OUT_TM
    row = lambda i: (i, 0)
    const2 = lambda i: (0, 0)
    const3 = lambda i: (0, 0, 0)
    return pl.pallas_call(
        functools.partial(_out_proj_kernel, period=period),
        out_shape=[
            jax.ShapeDtypeStruct((t, D_MODEL), F32),
            jax.ShapeDtypeStruct((t, PACKED), jnp.uint32),
            jax.ShapeDtypeStruct((t, LANES), jnp.int32),
            jax.ShapeDtypeStruct((t, LANES), F32),
        ],
        grid=(t // tm,),
        in_specs=[
            pl.BlockSpec((tm, D_MODEL), row),
            pl.BlockSpec((tm, SB_WIDTH), row),
            pl.BlockSpec((tm, GM_WIDTH), row),
            pl.BlockSpec((tm, GM_WIDTH), row),
            pl.BlockSpec((GM_GROUPS, CHUNK, CHUNK), const3),
            pl.BlockSpec((GM_GROUPS, CHUNK, 1), const3),
            pl.BlockSpec((1, SB_WIDTH), const2),
            pl.BlockSpec((1, GM_WIDTH), const2),
            pl.BlockSpec((D_MODEL, D_MODEL), const2),
            pl.BlockSpec((1, D_MODEL), const2),
            pl.BlockSpec((D_MODEL, LANES), const2),
            pl.BlockSpec((D_MODEL, LANES), const2),
            pl.BlockSpec((1, LANES), const2),
        ],
        out_specs=[
            pl.BlockSpec((tm, D_MODEL), row),
            pl.BlockSpec((tm, PACKED), row),
            pl.BlockSpec((tm, LANES), row),
            pl.BlockSpec((tm, LANES), row),
        ],
        scratch_shapes=[pltpu.VMEM((tm, GM_WIDTH), F32), pltpu.VMEM((tm, D_MODEL), BF16)],
        compiler_params=_params(("arbitrary",)),
        name="out_proj",
    )(x, sb_o, u, vn, wm, bm, g_sb, g_gm, w_out_bf, g_ffn, wr_hi, wr_lo, b_r)


def _gather_kernel(dst_ref, nu_ref, idx_ref, table_ref, o_ref, sem, *, tm):
    def row_copy(r, src_row):
        return pltpu.make_async_copy(table_ref.at[pl.ds(src_row, 1), :], o_ref.at[pl.ds(r, 1), :], sem)

    @pl.when(pl.program_id(0) < nu_ref[0])
    def _():
        def start(p, _):
            for prio in range(2):
                r = 2 * p + prio
                row_copy(r, idx_ref[0, 0, r]).start(priority=prio)
            return 0

        def wait(r, _):
            row_copy(r, 0).wait()
            return 0

        lax.fori_loop(0, tm // 2, start, 0, unroll=4)
        lax.fori_loop(0, tm, wait, 0, unroll=8)


def _gather_rows(table, idx, dst, n_used, n_out_tiles):
    tm = MOE_SUB
    steps = idx.shape[0] // tm
    width = table.shape[1]
    grid_spec = pltpu.PrefetchScalarGridSpec(
        num_scalar_prefetch=2,
        grid=(steps,),
        in_specs=[
            pl.BlockSpec((1, 1, tm), lambda g, dst, nu: (g, 0, 0), memory_space=pltpu.SMEM),
            pl.BlockSpec(memory_space=pl.ANY),
        ],
        out_specs=pl.BlockSpec((tm, width), lambda g, dst, nu: (dst[g], 0)),
        scratch_shapes=[pltpu.SemaphoreType.DMA],
    )
    return pl.pallas_call(
        functools.partial(_gather_kernel, tm=tm),
        out_shape=jax.ShapeDtypeStruct((n_out_tiles * tm, width), table.dtype),
        grid_spec=grid_spec,
        compiler_params=_params(("arbitrary",)),
        name="gather_rows",
    )(dst, n_used, idx.reshape(steps, 1, tm), table)


def _moe_kernel(we_ref, blk_ref, nsub_ref, xs_ref, w1g_ref, w1l_ref, b1g_ref, b1l_ref, w2_ref, b2_ref,
                o_ref, x_scr, wg_scr, wl_scr, w2_scr):
    c = pl.program_id(1)
    nsub = nsub_ref[pl.program_id(0)]
    sub = MOE_SUB

    @pl.when(nsub > 0)
    def _():
        wg_scr[...] = w1g_ref[0].astype(BF16)
        wl_scr[...] = w1l_ref[0].astype(BF16)
        w2_scr[...] = w2_ref[0].astype(BF16)

        def sub_rows(s):
            return pl.ds(pl.multiple_of(s * sub, sub), sub)

        @pl.when(c == 0)
        def _():
            def prepare(s, _):
                words = xs_ref[sub_rows(s), :]
                x_scr[sub_rows(s), :PACKED] = lax.bitcast_convert_type(words << 16, F32).astype(BF16)
                x_scr[sub_rows(s), PACKED:] = lax.bitcast_convert_type(
                    words & jnp.uint32(0xFFFF0000), F32).astype(BF16)
                o_ref[sub_rows(s), :] = jnp.broadcast_to(b2_ref[0], (sub, D_MODEL))
                return 0

            def zero(s, _):
                o_ref[sub_rows(s), :] = jnp.zeros((sub, D_MODEL), F32)
                return 0

            lax.fori_loop(0, nsub, prepare, 0)
            lax.fori_loop(nsub, MOE_R // sub, zero, 0)

        def hidden(s):
            x = x_scr[sub_rows(s), :]
            return (jnp.dot(x, wg_scr[...], preferred_element_type=F32) + b1g_ref[0],
                    jnp.dot(x, wl_scr[...], preferred_element_type=F32) + b1l_ref[0])

        def rows(s0, groups):
            h = hidden(s0)
            for g in range(groups):
                nxt = hidden(s0 + g + 1) if g + 1 < groups else None
                glu = jnp.minimum(h[0], SWIGLU_LIMIT)
                lin = jnp.clip(h[1], -SWIGLU_LIMIT, SWIGLU_LIMIT)
                act = glu * jax.nn.sigmoid(SWIGLU_ALPHA * glu) * (lin + 1.0)
                o_ref[sub_rows(s0 + g), :] += jnp.dot(act.astype(BF16), w2_scr[...], preferred_element_type=F32)
                h = nxt

        quads = nsub // 4

        def quad(p, _):
            rows(p * 4, 4)
            return 0

        lax.fori_loop(0, quads, quad, 0)

        @pl.when(nsub % 4 >= 2)
        def _():
            rows(quads * 4, 2)

        @pl.when(nsub % 2 == 1)
        def _():
            rows(nsub - 1, 1)


def _moe(xs, item_exp, item_blk, item_nsub, w1, b1, w2, b2):
    r, th = MOE_R, MOE_TH
    nc = D_EXPERT // th
    n_items = item_exp.shape[0]

    def chunk(w, c, ns):
        return jnp.where(ns[w] > 0, c, nc - 1)

    grid_spec = pltpu.PrefetchScalarGridSpec(
        num_scalar_prefetch=3,
        grid=(n_items, nc),
        in_specs=[
            pl.BlockSpec((r, PACKED), lambda w, c, we, blk, ns: (blk[w], 0)),
            pl.BlockSpec((1, D_MODEL, th), lambda w, c, we, blk, ns: (we[w], 0, chunk(w, c, ns))),
            pl.BlockSpec((1, D_MODEL, th), lambda w, c, we, blk, ns: (we[w], 0, nc + chunk(w, c, ns))),
            pl.BlockSpec((1, 1, th), lambda w, c, we, blk, ns: (we[w], 0, chunk(w, c, ns))),
            pl.BlockSpec((1, 1, th), lambda w, c, we, blk, ns: (we[w], 0, nc + chunk(w, c, ns))),
            pl.BlockSpec((1, th, D_MODEL), lambda w, c, we, blk, ns: (we[w], chunk(w, c, ns), 0)),
            pl.BlockSpec((1, 1, D_MODEL), lambda w, c, we, blk, ns: (we[w], 0, 0)),
        ],
        out_specs=pl.BlockSpec((r, D_MODEL), lambda w, c, we, blk, ns: (blk[w], 0),
                               pipeline_mode=pl.Buffered(1)),
        scratch_shapes=[
            pltpu.VMEM((r, D_MODEL), BF16),
            pltpu.VMEM((D_MODEL, th), BF16),
            pltpu.VMEM((D_MODEL, th), BF16),
            pltpu.VMEM((th, D_MODEL), BF16),
        ],
    )
    return pl.pallas_call(
        _moe_kernel,
        out_shape=jax.ShapeDtypeStruct((n_items * r, D_MODEL), F32),
        grid_spec=grid_spec,
        compiler_params=_params(("arbitrary", "arbitrary")),
        name="moe_experts",
    )(item_exp, item_blk, item_nsub, xs, w1, w1, b1, b1, w2, b2)


def _final_kernel(h_ref, y0_ref, y1_ref, y2_ref, y3_ref, gate_ref, p_ref, gple_ref, wg_ref, wp_ref,
                  gfin_ref, o_ref):
    gates = gate_ref[...]
    h = h_ref[...]
    for k, y_ref in enumerate((y0_ref, y1_ref, y2_ref, y3_ref)):
        h = h + y_ref[...] * gates[:, k:k + 1]
    a = _rms(h, gple_ref[...]).astype(BF16)
    gate = jax.nn.sigmoid(jnp.dot(a, wg_ref[...], preferred_element_type=F32))
    proj = jnp.dot(p_ref[...].astype(BF16), wp_ref[...], preferred_element_type=F32)
    o_ref[...] = _rms(h + proj * gate, gfin_ref[...])


def _final(h, yg, gates, p, g_ple, w_gate_bf, w_proj_bf, g_final):
    t = h.shape[0]
    tm = FIN_TM
    nt = t // tm
    row = lambda i: (i, 0)
    const2 = lambda i: (0, 0)
    y_specs = [pl.BlockSpec((tm, D_MODEL), lambda i, k=k: (k * nt + i, 0)) for k in range(TOP_K)]
    return pl.pallas_call(
        _final_kernel,
        out_shape=jax.ShapeDtypeStruct((t, D_MODEL), F32),
        grid=(nt,),
        in_specs=[pl.BlockSpec((tm, D_MODEL), row)] + y_specs + [
            pl.BlockSpec((tm, LANES), row),
            pl.BlockSpec((tm, PLE_DIM), row),
            pl.BlockSpec((1, D_MODEL), const2),
            pl.BlockSpec((D_MODEL, D_MODEL), const2),
            pl.BlockSpec((PLE_DIM, D_MODEL), const2),
            pl.BlockSpec((1, D_MODEL), const2),
        ],
        out_specs=pl.BlockSpec((tm, D_MODEL), row),
        compiler_params=_params(("arbitrary",)),
        name="final",
    )(h, yg, yg, yg, yg, gates, p, g_ple, w_gate_bf, w_proj_bf, g_final)


def _route(top_idx):
    t = top_idx.shape[0]
    n = t * TOP_K
    r, sub = MOE_R, MOE_SUB
    e_flat = top_idx.reshape(-1)
    tok_flat = jnp.repeat(jnp.arange(t, dtype=jnp.int32), TOP_K)
    onehot = (e_flat[:, None] == jnp.arange(N_EXPERTS, dtype=jnp.int32)[None, :]).astype(jnp.int32)
    csum = jnp.cumsum(onehot, axis=0)
    rank = jnp.sum((csum - onehot) * onehot, axis=1)
    counts = csum[-1]
    items_per = (counts + r - 1) // r
    item_end = jnp.cumsum(items_per)
    item_start = item_end - items_per
    n_items = item_end[-1]
    max_items = -(-n // r) + N_EXPERTS
    w = jnp.minimum(jnp.arange(max_items, dtype=jnp.int32), jnp.maximum(n_items - 1, 0))
    item_exp = jnp.clip(jnp.searchsorted(item_end, w, side='right'), 0, N_EXPERTS - 1).astype(jnp.int32)
    rows_in = jnp.clip(counts[item_exp] - (w - item_start[item_exp]) * r, 0, r)
    live = jnp.arange(max_items, dtype=jnp.int32) < n_items
    item_nsub = jnp.where(live, (rows_in + sub - 1) // sub, 0).astype(jnp.int32)
    item_blk = w.astype(jnp.int32)

    row_item = jnp.sum(onehot * item_start[None, :], axis=1) + rank // r
    in_item = rank % r
    pos = row_item * r + in_item

    sub_end = jnp.cumsum(item_nsub)
    sub_start = sub_end - item_nsub
    n_sub = sub_end[-1]
    max_sub = -(-n // sub) + N_EXPERTS
    g = jnp.minimum(jnp.arange(max_sub, dtype=jnp.int32), jnp.maximum(n_sub - 1, 0))
    g_item = jnp.clip(jnp.searchsorted(sub_end, g, side='right'), 0, max_items - 1).astype(jnp.int32)
    sub_dst = (g_item * (r // sub) + g - sub_start[g_item]).astype(jnp.int32)
    cpos = (sub_start[row_item] + in_item // sub) * sub + in_item % sub
    src_tok = (jnp.arange(max_sub * sub, dtype=jnp.int32) % t).at[cpos].set(tok_flat)
    pos_slot_major = pos.reshape(t, TOP_K).T.reshape(-1).astype(jnp.int32)
    return src_tok, sub_dst, n_sub.astype(jnp.int32).reshape(1), pos_slot_major, item_exp, item_blk, item_nsub


def kernel(x_prompt, x_sample, cache_k, cache_v, page_table, p_prompt, p_sample, g_mix, w_in, sb_bias,
           gm_ln_g, gm_ln_b, gm_w_s, gm_b_s, g_sb_out, g_gm_out, w_out, g_ffn, w_router, b_router,
           w_exp1, b_exp1, w_exp2, b_exp2, g_ple, w_ple_gate, w_ple_proj, g_final):
    assert g_mix.shape[0] == 1, "single-layer step only"
    batch, seq, _ = x_prompt.shape
    dec_b, dec_seq, _ = x_sample.shape
    n_phys = cache_k.shape[1]
    tp, ts = batch * seq, dec_b * dec_seq
    hp = x_prompt.reshape(tp, D_MODEL)
    hs = x_sample.reshape(ts, D_MODEL)

    w_in_bf = w_in.reshape(D_MODEL, N_SECTIONS * SB_WIDTH).astype(BF16)
    w_out_bf = w_out.reshape(D_MODEL, D_MODEL).astype(BF16)
    w_gate_bf = w_ple_gate.reshape(D_MODEL, D_MODEL).astype(BF16)
    w_proj_bf = w_ple_proj.reshape(PLE_DIM, D_MODEL).astype(BF16)
    gmix = g_mix.reshape(1, D_MODEL)
    ln_g = gm_ln_g.reshape(1, GM_WIDTH)
    ln_b = gm_ln_b.reshape(1, GM_WIDTH)
    g_sb = g_sb_out.reshape(1, SB_WIDTH)
    g_gm = g_gm_out.reshape(1, GM_WIDTH)
    gffn = g_ffn.reshape(1, D_MODEL)
    gple = g_ple.reshape(1, D_MODEL)
    gfin = g_final.reshape(1, D_MODEL)
    bias = sb_bias.reshape(SB_HEADS)
    wr = jnp.pad(w_router.reshape(D_MODEL, N_EXPERTS), ((0, 0), (0, LANES - N_EXPERTS)))
    wr_hi = wr.astype(BF16)
    wr_lo = (wr - wr_hi.astype(F32)).astype(BF16)
    b_r = jnp.pad(b_router.reshape(N_EXPERTS), (0, LANES - N_EXPERTS),
                  constant_values=-jnp.inf).reshape(1, LANES)
    w_s = gm_w_s.reshape(GM_GROUPS, CHUNK, CHUNK)
    b_s = gm_b_s.reshape(GM_GROUPS, CHUNK)
    reps = CHUNK // dec_seq
    wm_s = jnp.tile(w_s[:, :dec_seq, :dec_seq], (1, reps, reps))
    bm_s = jnp.tile(b_s[:, :dec_seq], (1, reps)).reshape(GM_GROUPS, CHUNK, 1)
    ck = cache_k.reshape(n_phys, PAGE_SIZE, SB_HEADS, HEAD_DIM)
    cv = cache_v.reshape(n_phys, PAGE_SIZE, SB_HEADS, HEAD_DIM)

    qp, kp, vp, kpb, vpb, up, vnp = _in_proj(hp, gmix, w_in_bf, ln_g, ln_b, BF16, BF16)
    sbo_p = _sb_prompt(qp, kpb, vpb, bias, batch, seq)
    h1p, fp, idxp, gatep = _out_proj(hp, sbo_p, up, vnp, w_s, b_s.reshape(GM_GROUPS, CHUNK, 1), g_sb, g_gm,
                                     w_out_bf, gffn, wr_hi, wr_lo, b_r, CHUNK)
    qs, ksn, vsn, _, _, us, vns = _in_proj(hs, gmix, w_in_bf, ln_g, ln_b, F32, F32)
    sbo_s = _sb_sample(qs, ksn, vsn, ck, cv, page_table, bias, dec_seq)
    h1s, fs, idxs, gates_s = _out_proj(hs, sbo_s, us, vns, wm_s, bm_s, g_sb, g_gm, w_out_bf, gffn,
                                       wr_hi, wr_lo, b_r, dec_seq)

    f_all = jnp.concatenate([fp, fs], axis=0)
    h1 = jnp.concatenate([h1p, h1s], axis=0)
    gates = jnp.concatenate([gatep, gates_s], axis=0)
    top_idx = jnp.concatenate([idxp, idxs], axis=0)[:, :TOP_K]
    src_tok, sub_dst, n_sub, pos, item_exp, item_blk, item_nsub = _route(top_idx)
    n_items = item_exp.shape[0]
    xs = _gather_rows(f_all, src_tok, sub_dst, n_sub, n_items * (MOE_R // MOE_SUB))
    yb = _moe(xs, item_exp, item_blk, item_nsub,
              w_exp1.reshape(N_EXPERTS, D_MODEL, 2 * D_EXPERT), b_exp1.reshape(N_EXPERTS, 1, 2 * D_EXPERT),
              w_exp2.reshape(N_EXPERTS, D_EXPERT, D_MODEL), b_exp2.reshape(N_EXPERTS, 1, D_MODEL))
    n_steps = pos.shape[0] // MOE_SUB
    yg = _gather_rows(yb, pos, jnp.arange(n_steps, dtype=jnp.int32), jnp.full((1,), n_steps, jnp.int32), n_steps)
    p_all = jnp.concatenate([p_prompt.reshape(tp, PLE_DIM), p_sample.reshape(ts, PLE_DIM)], axis=0)
    y = _final(h1, yg, gates, p_all, gple, w_gate_bf, w_proj_bf, gfin)

    kv_p = (1, batch, seq, SB_HEADS, HEAD_DIM)
    kv_s = (1, dec_b, dec_seq, SB_HEADS, HEAD_DIM)
    return (y[:tp].reshape(batch, seq, D_MODEL), y[tp:].reshape(dec_b, dec_seq, D_MODEL),
            kp.reshape(kv_p), vp.reshape(kv_p), ksn.reshape(kv_s), vsn.reshape(kv_s),
            vns.reshape(1, dec_b, dec_seq, GM_GROUPS, GM_CH))
```
